```python
import jax, jax.numpy as jnp
from jax import lax
import numpy as np

D_MODEL = 2048
BATCH = 4
SEQ = 4096
DEPTH = 2

FOURIER_GROUPS = 4
FOURIER_GROUP_DIM = 256
FOURIER_WIDTH = FOURIER_GROUPS * FOURIER_GROUP_DIM
DN_HEADS = 16
DN_HEAD_DIM = 128
DN_WIDTH = DN_HEADS * DN_HEAD_DIM
CONV_K = 5
CHUNK = 64
D_FF_DENSE = 5632
N_EXPERTS = 8
TOP_K = 2
D_FF_EXPERT = 7168
PLE_DIM = 256
DEEPNORM_ALPHA = (2 * DEPTH) ** 0.25
DEEPNORM_BETA = (8 * DEPTH) ** -0.25
LN_EPS = 1e-5
RMS_EPS = 1e-6
L2_EPS = 1e-6
N_DENSE = (DEPTH + 1) // 2
N_MOE = DEPTH // 2
IN_SPLITS = (FOURIER_WIDTH, 3 * DN_WIDTH, DN_WIDTH, 2 * DN_HEADS, 2 * DN_HEADS, D_MODEL, D_MODEL)
IN_WIDTH = sum(IN_SPLITS)
IN_SPLIT_IDX = tuple(sum(IN_SPLITS[:j]) for j in range(1, len(IN_SPLITS)))

kernel_name = "hybrid_fnet_gdn_moe_deepnorm_encoder"


def layer_norm(x, g, b):
    xf = x.astype(jnp.float32)
    mu = jnp.mean(xf, -1, keepdims=True)
    var = jnp.mean(jnp.square(xf - mu), -1, keepdims=True)
    return ((xf - mu) * lax.rsqrt(var + LN_EPS) * g.astype(jnp.float32) + b.astype(jnp.float32)).astype(x.dtype)


def l2norm(t):
    return t * lax.rsqrt(jnp.sum(t * t, -1, keepdims=True) + L2_EPS)


def centred_dwconv(u, w):
    pad = (CONV_K - 1) // 2
    return lax.conv_general_dilated(
        u, w[:, None, :].astype(u.dtype), window_strides=(1,), padding=((pad, pad),),
        dimension_numbers=("NWC", "WIO", "NWC"), feature_group_count=u.shape[-1])


def fourier_mix(u):
    b, s, _ = u.shape
    ug = u.astype(jnp.float32).reshape(b, s, FOURIER_GROUPS, FOURIER_GROUP_DIM)
    f = jnp.fft.fft2(ug, axes=(1, 3), norm="ortho")
    return jnp.real(f).reshape(b, s, FOURIER_WIDTH).astype(u.dtype)


def gated_delta_chunked(q, k, v, g, beta):
    b, h, s, dk = q.shape
    dv = v.shape[-1]
    n = s // CHUNK
    qc = q.reshape(b, h, n, CHUNK, dk)
    kc = k.reshape(b, h, n, CHUNK, dk)
    vc = v.reshape(b, h, n, CHUNK, dv)
    bc = beta.reshape(b, h, n, CHUNK)
    gcum = jnp.cumsum(g.reshape(b, h, n, CHUNK), -1)
    causal = jnp.tril(jnp.ones((CHUNK, CHUNK), bool))
    strict = jnp.tril(jnp.ones((CHUNK, CHUNK), bool), -1)
    decay = jnp.exp(jnp.where(causal, gcum[..., :, None] - gcum[..., None, :], -jnp.inf))
    kk = jnp.einsum("bhnid,bhnjd->bhnij", kc, kc)
    a_mat = jnp.where(strict, bc[..., :, None] * kk * decay, 0.0) + jnp.eye(CHUNK, dtype=jnp.float32)
    rhs = jnp.concatenate([vc * bc[..., None], kc * (bc * jnp.exp(gcum))[..., None]], -1)
    sol = lax.linalg.triangular_solve(a_mat, rhs, left_side=True, lower=True, unit_diagonal=True)
    u_c, w_c = sol[..., :dv], sol[..., dv:]
    qk = jnp.where(causal, jnp.einsum("bhnid,bhnjd->bhnij", qc, kc) * decay, 0.0)
    q_dec = qc * jnp.exp(gcum)[..., None]
    k_dec = kc * jnp.exp(gcum[..., -1:] - gcum)[..., None]
    g_last = jnp.exp(gcum[..., -1])
    xs = tuple(jnp.moveaxis(t, 2, 0) for t in (u_c, w_c, qk, q_dec, k_dec, g_last))

    def step(state, inp):
        u_i, w_i, qk_i, qd_i, kd_i, gl_i = inp
        v_new = u_i - jnp.einsum("bhck,bhkv->bhcv", w_i, state)
        o_i = jnp.einsum("bhck,bhkv->bhcv", qd_i, state) + jnp.einsum("bhij,bhjv->bhiv", qk_i, v_new)
        state = state * gl_i[..., None, None] + jnp.einsum("bhck,bhcv->bhkv", kd_i, v_new)
        return state, o_i

    s0 = jnp.zeros((b, h, dk, dv), jnp.float32)
    _, o = lax.scan(step, s0, xs)
    return jnp.moveaxis(o, 0, 2).reshape(b, h, s, dv)


def deltanet_branch(qkv, z, beta_raw, a_raw, conv_w, a_log, dt_bias, o_norm_w):
    b, s, _ = z.shape
    qkv = jax.nn.silu(centred_dwconv(qkv, conv_w))
    q, k, v = jnp.split(qkv, 3, axis=-1)

    def heads(t):
        return t.reshape(b, s, DN_HEADS, DN_HEAD_DIM).transpose(0, 2, 1, 3).astype(jnp.float32)

    q = l2norm(heads(q)) * (DN_HEAD_DIM ** -0.5)
    k = l2norm(heads(k))
    v = heads(v)
    beta = jax.nn.sigmoid(beta_raw.astype(jnp.float32)).reshape(b, s, 2, DN_HEADS).transpose(2, 0, 3, 1)
    a_in = a_raw.astype(jnp.float32).reshape(b, s, 2, DN_HEADS).transpose(2, 0, 3, 1)
    g = -jnp.exp(a_log.astype(jnp.float32))[:, None, :, None] * jax.nn.softplus(
        a_in + dt_bias.astype(jnp.float32)[:, None, :, None])
    o_fwd = gated_delta_chunked(q, k, v, g[0], beta[0])
    rev = lambda t: jnp.flip(t, axis=2)
    o_bwd = rev(gated_delta_chunked(rev(q), rev(k), rev(v), rev(g[1]), rev(beta[1])))
    o = (o_fwd + o_bwd).transpose(0, 2, 1, 3)
    zh = z.astype(jnp.float32).reshape(b, s, DN_HEADS, DN_HEAD_DIM)
    o = o * lax.rsqrt(jnp.mean(o * o, -1, keepdims=True) + RMS_EPS) * o_norm_w.astype(jnp.float32) * jax.nn.silu(zh)
    return o.reshape(b, s, DN_WIDTH).astype(z.dtype)


def swiglu(t, w_gate_up, w_down):
    gu = t @ w_gate_up
    gate, up = jnp.split(gu, 2, axis=-1)
    return (jax.nn.silu(gate) * up) @ w_down


def moe_swiglu(x, router_w, e_gate_up, e_down):
    b, s, d = x.shape
    t = x.reshape(b * s, d)
    logits = (t @ router_w).astype(jnp.float32)
    top_v, top_i = lax.top_k(logits, TOP_K)
    top_w = jax.nn.softmax(top_v, axis=-1)
    combine = jnp.sum(jax.nn.one_hot(top_i, N_EXPERTS, dtype=jnp.float32) * top_w[..., None], axis=1)
    out = jnp.zeros_like(t)
    for e in range(N_EXPERTS):
        out = out + combine[:, e:e + 1].astype(t.dtype) * swiglu(t, e_gate_up[e], e_down[e])
    return out.reshape(b, s, d)


def setup_inputs(seed: int = 0) -> dict:
    key = jax.random.key(seed)
    ks = jax.random.split(key, 26)
    f32 = jnp.float32
    nrm = lambda k, shape, scale: jax.random.normal(k, shape, f32) * scale
    dt = jnp.exp(jax.random.uniform(ks[5], (DEPTH, 2, DN_HEADS), f32, np.log(1e-3), np.log(1e-1)))
    return {
        "x": nrm(ks[0], (BATCH, SEQ, D_MODEL), 1.0),
        "p": nrm(ks[1], (DEPTH, BATCH, SEQ, PLE_DIM), 1.0),
        "emb_ln_g": 1.0 + nrm(ks[2], (D_MODEL,), 0.01),
        "emb_ln_b": nrm(ks[3], (D_MODEL,), 0.01),
        "w_in": nrm(ks[4], (DEPTH, D_MODEL, IN_WIDTH), D_MODEL ** -0.5),
        "conv_w": nrm(ks[6], (DEPTH, CONV_K, 3 * DN_WIDTH), CONV_K ** -0.5),
        "a_log": jnp.log(jax.random.uniform(ks[7], (DEPTH, 2, DN_HEADS), f32, 1.0, 16.0)),
        "dt_bias": dt + jnp.log(-jnp.expm1(-dt)),
        "o_norm_w": 1.0 + nrm(ks[8], (DEPTH, DN_HEAD_DIM), 0.01),
        "w_fourier": nrm(ks[9], (DEPTH, FOURIER_WIDTH, D_MODEL), DEEPNORM_BETA * FOURIER_WIDTH ** -0.5),
        "w_delta": nrm(ks[10], (DEPTH, DN_WIDTH, D_MODEL), DEEPNORM_BETA * DN_WIDTH ** -0.5),
        "w_out": nrm(ks[11], (DEPTH, D_MODEL, D_MODEL), DEEPNORM_BETA * D_MODEL ** -0.5),
        "ln1_g": 1.0 + nrm(ks[12], (DEPTH, D_MODEL), 0.01),
        "ln1_b": nrm(ks[13], (DEPTH, D_MODEL), 0.01),
        "ffn_gate_up": nrm(ks[14], (N_DENSE, D_MODEL, 2 * D_FF_DENSE), D_MODEL ** -0.5),
        "ffn_down": nrm(ks[15], (N_DENSE, D_FF_DENSE, D_MODEL), DEEPNORM_BETA * D_FF_DENSE ** -0.5),
        "router_w": nrm(ks[16], (N_MOE, D_MODEL, N_EXPERTS), D_MODEL ** -0.5),
        "exp_gate_up": nrm(ks[17], (N_MOE, N_EXPERTS, D_MODEL, 2 * D_FF_EXPERT), D_MODEL ** -0.5),
        "exp_down": nrm(ks[18], (N_MOE, N_EXPERTS, D_FF_EXPERT, D_MODEL), DEEPNORM_BETA * D_FF_EXPERT ** -0.5),
        "ple_gate": nrm(ks[19], (DEPTH, D_MODEL, D_MODEL), D_MODEL ** -0.5),
        "ple_proj": nrm(ks[20], (DEPTH, PLE_DIM, D_MODEL), DEEPNORM_BETA * PLE_DIM ** -0.5),
        "ln2_g": 1.0 + nrm(ks[21], (DEPTH, D_MODEL), 0.01),
        "ln2_b": nrm(ks[22], (DEPTH, D_MODEL), 0.01),
    }


def reference(x, p, emb_ln_g, emb_ln_b, w_in, conv_w, a_log, dt_bias, o_norm_w, w_fourier, w_delta,
              w_out, ln1_g, ln1_b, ffn_gate_up, ffn_down, router_w, exp_gate_up, exp_down,
              ple_gate, ple_proj, ln2_g, ln2_b):
    x = layer_norm(x, emb_ln_g, emb_ln_b)
    for i in range(DEPTH):
        proj = x @ w_in[i]
        u_f, qkv, z, beta_raw, a_raw, gate_f, gate_d = jnp.split(proj, IN_SPLIT_IDX, axis=-1)
        y_f = fourier_mix(u_f) @ w_fourier[i]
        y_d = deltanet_branch(qkv, z, beta_raw, a_raw, conv_w[i], a_log[i], dt_bias[i], o_norm_w[i]) @ w_delta[i]
        merged = jax.nn.sigmoid(gate_f) * y_f + jax.nn.sigmoid(gate_d) * y_d
        mix = merged @ w_out[i]
        x = layer_norm(DEEPNORM_ALPHA * x + mix, ln1_g[i], ln1_b[i])
        if i % 2 == 0:
            ff = swiglu(x, ffn_gate_up[i // 2], ffn_down[i // 2])
        else:
            ff = moe_swiglu(x, router_w[i // 2], exp_gate_up[i // 2], exp_down[i // 2])
        ple = jax.nn.sigmoid(x @ ple_gate[i]) * (p[i] @ ple_proj[i])
        x = layer_norm(DEEPNORM_ALPHA * x + ff + ple, ln2_g[i], ln2_b[i])
    return x
```

```python
import functools
import math

import numpy as np
import jax
import jax.numpy as jnp
from jax import lax
from jax.experimental import pallas as pl
from jax.experimental.pallas import tpu as pltpu

F32 = jnp.float32
BF16 = jnp.bfloat16
HIGHEST = lax.Precision.HIGHEST

FOURIER_GROUPS = 4
FOURIER_GROUP_DIM = 256
FOURIER_WIDTH = FOURIER_GROUPS * FOURIER_GROUP_DIM
DN_HEADS = 16
DN_HEAD_DIM = 128
DN_WIDTH = DN_HEADS * DN_HEAD_DIM
CONV_K = 5
CHUNK = 64
N_EXPERTS = 8
TOP_K = 2
LN_EPS = 1e-5
RMS_EPS = 1e-6
L2_EPS = 1e-6
LANES = 128
VMEM_LIMIT = 56 * 1024 * 1024


def _cparams(sem):
    return pltpu.CompilerParams(dimension_semantics=sem, vmem_limit_bytes=VMEM_LIMIT)


def _dot(a, b):
    return jnp.dot(a, b, preferred_element_type=F32)


def _dot_nt(a, b):
    return lax.dot_general(a, b, (((1,), (1,)), ((), ())), preferred_element_type=F32)


def _dot_tn(a, b):
    return lax.dot_general(a, b, (((0,), (0,)), ((), ())), preferred_element_type=F32)


def _sigmoid(v):
    return 1.0 / (1.0 + jnp.exp(-v))


def _silu(v):
    return v * _sigmoid(v)


def _ln_rows(v, g, b):
    mu = jnp.mean(v, -1, keepdims=True)
    d = v - mu
    var = jnp.mean(d * d, -1, keepdims=True)
    return d * lax.rsqrt(var + LN_EPS) * g + b


def _ln_kernel(x_ref, g_ref, b_ref, of_ref, ob_ref):
    y = _ln_rows(x_ref[...], g_ref[...], b_ref[...])
    of_ref[...] = y
    ob_ref[...] = y.astype(BF16)


def layer_norm(x, g, b, *, tm=512):
    t, d = x.shape
    tm = min(tm, t)
    row = pl.BlockSpec((tm, d), lambda i: (i, 0))
    vec = pl.BlockSpec((1, d), lambda i: (0, 0))
    return pl.pallas_call(
        _ln_kernel,
        grid=(t // tm,),
        in_specs=[row, vec, vec],
        out_specs=[row, row],
        out_shape=[jax.ShapeDtypeStruct((t, d), F32), jax.ShapeDtypeStruct((t, d), BF16)],
        compiler_params=_cparams(("parallel",)),
    )(x, g.reshape(1, d), b.reshape(1, d))


def _mm_kernel(a_ref, w_ref, o_ref, *scratch, nk):
    if nk == 1:
        o_ref[...] = _dot(a_ref[...].astype(BF16), w_ref[...]).astype(o_ref.dtype)
        return
    acc_ref, = scratch
    k = pl.program_id(3)

    @pl.when(k == 0)
    def _():
        acc_ref[...] = jnp.zeros_like(acc_ref)

    acc_ref[...] += _dot(a_ref[...].astype(BF16), w_ref[...])

    @pl.when(k == nk - 1)
    def _():
        o_ref[...] = acc_ref[...].astype(o_ref.dtype)


def matmul(a, w, *, out_dtype, tm, tn, tk=None, a_k0=0):
    batched = w.ndim == 3
    if not batched:
        w = w[None]
    nb, kdim, n = w.shape
    m = a.shape[0]
    tm, tn = min(tm, m), min(tn, n)
    tk = kdim if tk is None else min(tk, kdim)
    nk = kdim // tk
    k0 = a_k0 // tk
    out = pl.pallas_call(
        functools.partial(_mm_kernel, nk=nk),
        grid=(nb, m // tm, n // tn, nk),
        in_specs=[pl.BlockSpec((tm, tk), lambda b, i, j, k: (i, k0 + k)),
                  pl.BlockSpec((None, tk, tn), lambda b, i, j, k: (b, k, j))],
        out_specs=pl.BlockSpec((None, tm, tn), lambda b, i, j, k: (b, i, j)),
        out_shape=jax.ShapeDtypeStruct((nb, m, n), out_dtype),
        scratch_shapes=[] if nk == 1 else [pltpu.VMEM((tm, tn), F32)],
        compiler_params=_cparams(("parallel", "parallel", "parallel", "arbitrary")),
    )(a, w)
    return out if batched else out[0]


def _gates_kernel(x_ref, wb_ref, wa_ref, alog_ref, dtb_ref, beta_ref, gc_ref, *, tm):
    x = x_ref[...]
    braw = jnp.dot(x, wb_ref[...], precision=HIGHEST, preferred_element_type=F32)
    araw = jnp.dot(x, wa_ref[...], precision=HIGHEST, preferred_element_type=F32)
    beta_ref[...] = _sigmoid(braw)
    zz = araw + dtb_ref[...]
    softplus = jnp.maximum(zz, 0.0) + jnp.log(1.0 + jnp.exp(-jnp.abs(zz)))
    g = -jnp.exp(alog_ref[...]) * softplus
    pos = lax.broadcasted_iota(jnp.int32, (tm, LANES), 0) % CHUNK
    lane = lax.broadcasted_iota(jnp.int32, (tm, LANES), 1)
    pre = g
    suf = g
    s = 1
    while s < CHUNK:
        pre = pre + jnp.where(pos >= s, pltpu.roll(pre, s, 0), 0.0)
        suf = suf + jnp.where(pos < CHUNK - s, pltpu.roll(suf, tm - s, 0), 0.0)
        s *= 2
    gc_ref[...] = jnp.where(lane < DN_HEADS, pre, suf)


def dn_gates(xf, wb, wa, alog, dtb, *, tm=512):
    t, d = xf.shape
    tm = min(tm, t)
    row = pl.BlockSpec((tm, d), lambda i: (i, 0))
    wsp = pl.BlockSpec((d, LANES), lambda i: (0, 0))
    vec = pl.BlockSpec((1, LANES), lambda i: (0, 0))
    osp = pl.BlockSpec((tm, LANES), lambda i: (i, 0))
    return pl.pallas_call(
        functools.partial(_gates_kernel, tm=tm),
        grid=(t // tm,),
        in_specs=[row, wsp, wsp, vec, vec],
        out_specs=[osp, osp],
        out_shape=[jax.ShapeDtypeStruct((t, LANES), F32)] * 2,
        compiler_params=_cparams(("parallel",)),
    )(xf, wb, wa, alog, dtb)


_CONV_PAD = 8
_CONV_ROWS = 256


def _dn_prep_kernel(u_ref, w_ref, o_ref, pad_ref, *, seq):
    j = pl.program_id(1)
    half = (CONV_K - 1) // 2
    zeros = jnp.zeros((_CONV_PAD, LANES), F32)
    pad_ref[0:_CONV_PAD, :] = zeros
    pad_ref[_CONV_PAD + seq:_CONV_PAD + seq + _CONV_PAD, :] = zeros
    pad_ref[_CONV_PAD:_CONV_PAD + seq, :] = u_ref[...]
    w = w_ref[...]
    is_qk = j < 2 * DN_HEADS
    scale = jnp.where(j < DN_HEADS, DN_HEAD_DIM ** -0.5, 1.0).astype(F32)
    rows = min(_CONV_ROWS, seq)
    for r0 in range(0, seq, rows):
        acc = jnp.zeros((rows, LANES), F32)
        for t in range(CONV_K):
            lo = _CONV_PAD + r0 + t - half
            acc = acc + pad_ref[lo:lo + rows, :] * w[t:t + 1, :]
        y = _silu(acc)
        nrm = lax.rsqrt(jnp.sum(y * y, -1, keepdims=True) + L2_EPS) * scale
        o_ref[r0:r0 + rows, :] = y * jnp.where(is_qk, nrm, 1.0)


def dn_prep(proj, conv_w, *, batch, seq, col0):
    nblk = 3 * DN_HEADS
    c0 = col0 // LANES
    nsb = 1
    return pl.pallas_call(
        functools.partial(_dn_prep_kernel, seq=seq),
        grid=(batch, nblk),
        in_specs=[pl.BlockSpec((seq, LANES), lambda b, j: (b * nsb, c0 + j)),
                  pl.BlockSpec((CONV_K, LANES), lambda b, j: (0, j))],
        out_specs=pl.BlockSpec((seq, LANES), lambda b, j: (b * nsb, j)),
        out_shape=jax.ShapeDtypeStruct((batch * seq, 3 * DN_WIDTH), F32),
        scratch_shapes=[pltpu.VMEM((seq + 2 * _CONV_PAD, LANES), F32)],
        compiler_params=_cparams(("parallel", "parallel")),
    )(proj, conv_w)


def _dn_core_kernel(q_ref, k_ref, v_ref, z_ref, beta_ref, gc_ref, gtf_ref, gtb_ref, onw_ref, o_ref,
                    wq_s, u_s, kd_s, qk_s, gl_s, oacc, *, nchunk):
    h = pl.program_id(1)
    c2 = 2 * CHUNK
    lane = lax.broadcasted_iota(jnp.int32, (CHUNK, LANES), 1)
    ri = lax.broadcasted_iota(jnp.int32, (CHUNK, CHUNK), 0)
    ci = lax.broadcasted_iota(jnp.int32, (CHUNK, CHUNK), 1)
    gt_refs = (gtf_ref, gtb_ref)

    def local(c, carry):
        r0 = pl.multiple_of(c * CHUNK, CHUNK)
        kc = k_ref[pl.ds(r0, CHUNK), :]
        qc = q_ref[pl.ds(r0, CHUNK), :]
        vc = v_ref[pl.ds(r0, CHUNK), :]
        kb = kc.astype(BF16)
        kk = _dot_nt(kb, kb)
        qk = _dot_nt(qc.astype(BF16), kb)
        bblk = beta_ref[pl.ds(r0, CHUNK), :]
        gblk = gc_ref[pl.ds(r0, CHUNK), :]
        for d in (0, 1):
            sel = lane == (d * DN_HEADS + h)
            beta_col = jnp.sum(jnp.where(sel, bblk, 0.0), -1, keepdims=True)
            gc_col = jnp.sum(jnp.where(sel, gblk, 0.0), -1, keepdims=True)
            gc_row = gt_refs[d][pl.ds(c, 1), :]
            causal = (ri >= ci) if d == 0 else (ri <= ci)
            strict = (ri > ci) if d == 0 else (ri < ci)
            dec = jnp.exp(jnp.where(causal, gc_col - gc_row, -jnp.inf))
            mp = jnp.where(strict, -(beta_col * kk * dec), 0.0)
            nm = mp
            for _ in range(5):
                mpb = mp.astype(BF16)
                mp = _dot(mpb, mpb)
                nm = nm + mp + _dot(nm.astype(BF16), mp.astype(BF16))
            ecol = jnp.exp(gc_col)
            rhs = jnp.concatenate([vc * beta_col, kc * (beta_col * ecol)], axis=1)
            sol = rhs + _dot(nm.astype(BF16), rhs.astype(BF16))
            gc_last = gc_row[:, CHUNK - 1:CHUNK] if d == 0 else gc_row[:, 0:1]
            w0 = pl.multiple_of(c * c2, c2)
            wq_s[d, pl.ds(w0, CHUNK), :] = sol[:, DN_HEAD_DIM:].astype(BF16)
            wq_s[d, pl.ds(w0 + CHUNK, CHUNK), :] = (qc * ecol).astype(BF16)
            u_s[d, pl.ds(r0, CHUNK), :] = sol[:, :DN_HEAD_DIM]
            kd_s[d, pl.ds(r0, CHUNK), :] = (kc * jnp.exp(gc_last - gc_col)).astype(BF16)
            qk_s[d, pl.ds(r0, CHUNK), :] = jnp.where(causal, qk * dec, 0.0).astype(BF16)
            gl_s[d, pl.ds(c, 1), :] = jnp.broadcast_to(jnp.exp(gc_last), (1, LANES))
        return carry

    lax.fori_loop(0, nchunk, local, 0)
    oacc[...] = jnp.zeros_like(oacc)

    def scan_step(d, c, state):
        r0 = pl.multiple_of(c * CHUNK, CHUNK)
        w0 = pl.multiple_of(c * c2, c2)
        r = _dot(wq_s[d, pl.ds(w0, c2), :], state.astype(BF16))
        vnew = (u_s[d, pl.ds(r0, CHUNK), :] - r[:CHUNK]).astype(BF16)
        o = r[CHUNK:] + _dot(qk_s[d, pl.ds(r0, CHUNK), :], vnew)
        oacc[pl.ds(r0, CHUNK), :] += o
        return state * gl_s[d, pl.ds(c, 1), :] + _dot_tn(kd_s[d, pl.ds(r0, CHUNK), :], vnew)

    def scan(i, carry):
        sf, sb = carry
        sf = scan_step(0, i, sf)
        sb = scan_step(1, nchunk - 1 - i, sb)
        return sf, sb

    s0 = jnp.zeros((DN_HEAD_DIM, DN_HEAD_DIM), F32)
    lax.fori_loop(0, nchunk, scan, (s0, s0))

    onw = onw_ref[...]
    rows = min(256, nchunk * CHUNK)

    def norm(i, carry):
        r0 = pl.multiple_of(i * rows, rows)
        o = oacc[pl.ds(r0, rows), :]
        zz = z_ref[pl.ds(r0, rows), :]
        y = o * lax.rsqrt(jnp.mean(o * o, -1, keepdims=True) + RMS_EPS) * onw * _silu(zz)
        o_ref[pl.ds(r0, rows), :] = y.astype(o_ref.dtype)
        return carry

    lax.fori_loop(0, (nchunk * CHUNK) // rows, norm, 0)


def dn_core(qkvn, proj, beta, gc, gct, onw, *, batch, seq, z_col0):
    nchunk = seq // CHUNK
    zc = z_col0 // LANES
    blk = lambda off: pl.BlockSpec((seq, LANES), lambda b, h: (b, off + h))
    gsp = pl.BlockSpec((seq, LANES), lambda b, h: (b, 0))
    return pl.pallas_call(
        functools.partial(_dn_core_kernel, nchunk=nchunk),
        grid=(batch, DN_HEADS),
        in_specs=[blk(0), blk(DN_HEADS), blk(2 * DN_HEADS), blk(zc), gsp, gsp,
                  pl.BlockSpec((None, None, nchunk, CHUNK), lambda b, h: (b, h, 0, 0)),
                  pl.BlockSpec((None, None, nchunk, CHUNK), lambda b, h: (b, DN_HEADS + h, 0, 0)),
                  pl.BlockSpec((1, LANES), lambda b, h: (0, 0))],
        out_specs=pl.BlockSpec((seq, LANES), lambda b, h: (b, h)),
        out_shape=jax.ShapeDtypeStruct((batch * seq, DN_WIDTH), BF16),
        scratch_shapes=[pltpu.VMEM((2, 2 * seq, LANES), BF16),
                        pltpu.VMEM((2, seq, LANES), F32),
                        pltpu.VMEM((2, seq, LANES), BF16),
                        pltpu.VMEM((2, seq, CHUNK), BF16),
                        pltpu.VMEM((2, nchunk, LANES), F32),
                        pltpu.VMEM((seq, LANES), F32)],
        compiler_params=_cparams(("parallel", "parallel")),
    )(qkvn, qkvn, qkvn, proj, beta, gc, gct, gct, onw)


def _chan_dft_kernel(u_ref, t_ref, o_ref):
    r = _dot(u_ref[...].astype(BF16), t_ref[...])
    g = FOURIER_GROUP_DIM
    o_ref[0] = r[:, :g].astype(BF16)
    o_ref[1] = r[:, g:].astype(BF16)


def chan_dft(proj, table, *, batch, seq, tm=1024):
    tm = min(tm, seq)
    nsb = seq // tm
    g = FOURIER_GROUP_DIM
    return pl.pallas_call(
        _chan_dft_kernel,
        grid=(batch, nsb, FOURIER_GROUPS),
        in_specs=[pl.BlockSpec((tm, g), lambda b, i, j: (b * nsb + i, j)),
                  pl.BlockSpec((g, 2 * g), lambda b, i, j: (0, 0))],
        out_specs=pl.BlockSpec((None, 2, tm, g), lambda b, i, j: (b, 0, i, j)),
        out_shape=jax.ShapeDtypeStruct((batch, 2, seq, FOURIER_WIDTH), BF16),
        compiler_params=_cparams(("parallel", "parallel", "parallel")),
    )(proj, table)


def _dft_tables(seq):
    g = FOURIER_GROUP_DIM
    jc = np.arange(g)
    ang_c = 2.0 * np.pi * ((jc[:, None] * jc[None, :]) % g) / g
    sc = 1.0 / math.sqrt(g)
    chan = np.concatenate([np.cos(ang_c) * sc, -np.sin(ang_c) * sc], axis=1)
    js = jnp.arange(seq, dtype=jnp.int32)
    m = (js[:, None] * js[None, :]) % seq
    ang = m.astype(F32) * (2.0 * math.pi / seq)
    ss = 1.0 / math.sqrt(seq)
    seq_t = jnp.concatenate([jnp.cos(ang) * ss, jnp.sin(ang) * ss], axis=1).astype(BF16)
    return jnp.asarray(chan, dtype=BF16), seq_t


def _merge_kernel(af_ref, ad_ref, wf_ref, wd_ref, gf_ref, gd_ref, o_ref):
    yf = _dot(af_ref[...], wf_ref[...])
    yd = _dot(ad_ref[...], wd_ref[...])
    o_ref[...] = (_sigmoid(gf_ref[...]) * yf + _sigmoid(gd_ref[...]) * yd).astype(o_ref.dtype)


def merge_branches(af, ad, wf, wd, gates, *, tm=512, tn=512):
    t, kf = af.shape
    kd = ad.shape[1]
    n = wf.shape[1]
    tm, tn = min(tm, t), min(tn, n)
    nb = n // tn
    return pl.pallas_call(
        _merge_kernel,
        grid=(t // tm, nb),
        in_specs=[pl.BlockSpec((tm, kf), lambda i, j: (i, 0)),
                  pl.BlockSpec((tm, kd), lambda i, j: (i, 0)),
                  pl.BlockSpec((kf, tn), lambda i, j: (0, j)),
                  pl.BlockSpec((kd, tn), lambda i, j: (0, j)),
                  pl.BlockSpec((tm, tn), lambda i, j: (i, j)),
                  pl.BlockSpec((tm, tn), lambda i, j: (i, nb + j))],
        out_specs=pl.BlockSpec((tm, tn), lambda i, j: (i, j)),
        out_shape=jax.ShapeDtypeStruct((t, n), BF16),
        compiler_params=_cparams(("parallel", "parallel")),
    )(af, ad, wf, wd, gates, gates)


def _mm_res_ln_kernel(*refs, nk, alpha, has_extra):
    if has_extra:
        a_ref, w_ref, x_ref, e_ref, g_ref, b_ref, of_ref, ob_ref, acc_ref = refs
    else:
        a_ref, w_ref, x_ref, g_ref, b_ref, of_ref, ob_ref, acc_ref = refs
        e_ref = None
    k = pl.program_id(1)

    @pl.when(k == 0)
    def _():
        acc_ref[...] = jnp.zeros_like(acc_ref)

    acc_ref[...] += _dot(a_ref[...], w_ref[...])

    @pl.when(k == nk - 1)
    def _():
        v = alpha * x_ref[...] + acc_ref[...]
        if has_extra:
            v = v + e_ref[...]
        y = _ln_rows(v, g_ref[...], b_ref[...])
        of_ref[...] = y
        ob_ref[...] = y.astype(BF16)


def mm_res_ln(a, w, x, g, b, *, alpha, extra=None, tm=512, tk=512):
    t, kdim = a.shape
    d = w.shape[1]
    tm, tk = min(tm, t), min(tk, kdim)
    nk = kdim // tk
    row = pl.BlockSpec((tm, d), lambda i, k: (i, 0))
    vec = pl.BlockSpec((1, d), lambda i, k: (0, 0))
    in_specs = [pl.BlockSpec((tm, tk), lambda i, k: (i, k)),
                pl.BlockSpec((tk, d), lambda i, k: (k, 0)), row]
    args = [a, w, x]
    if extra is not None:
        in_specs.append(row)
        args.append(extra)
    in_specs += [vec, vec]
    args += [g.reshape(1, d), b.reshape(1, d)]
    return pl.pallas_call(
        functools.partial(_mm_res_ln_kernel, nk=nk, alpha=alpha, has_extra=extra is not None),
        grid=(t // tm, nk),
        in_specs=in_specs,
        out_specs=[row, row],
        out_shape=[jax.ShapeDtypeStruct((t, d), F32), jax.ShapeDtypeStruct((t, d), BF16)],
        scratch_shapes=[pltpu.VMEM((tm, d), F32)],
        compiler_params=_cparams(("parallel", "arbitrary")),
    )(*args)


def _swiglu_kernel(x_ref, wg_ref, wu_ref, o_ref):
    x = x_ref[...]
    o_ref[...] = (_silu(_dot(x, wg_ref[...])) * _dot(x, wu_ref[...])).astype(o_ref.dtype)


def swiglu_up(x, wgu, *, tm=1024, tn=512):
    t, d = x.shape
    f = wgu.shape[1] // 2
    tm, tn = min(tm, t), min(tn, f)
    nf = f // tn
    return pl.pallas_call(
        _swiglu_kernel,
        grid=(t // tm, nf),
        in_specs=[pl.BlockSpec((tm, d), lambda i, j: (i, 0)),
                  pl.BlockSpec((d, tn), lambda i, j: (0, j)),
                  pl.BlockSpec((d, tn), lambda i, j: (0, nf + j))],
        out_specs=pl.BlockSpec((tm, tn), lambda i, j: (i, j)),
        out_shape=jax.ShapeDtypeStruct((t, f), BF16),
        compiler_params=_cparams(("parallel", "parallel")),
    )(x, wgu, wgu)


def _ple_kernel(x_ref, p_ref, wg_ref, wp_ref, o_ref):
    o_ref[...] = (_sigmoid(_dot(x_ref[...], wg_ref[...]))
                  * _dot(p_ref[...].astype(BF16), wp_ref[...])).astype(o_ref.dtype)


def ple_term(xb, p, wpg, wpp, *, tm=512, tn=1024):
    t, d = xb.shape
    pd = p.shape[1]
    n = wpg.shape[1]
    tm, tn = min(tm, t), min(tn, n)
    return pl.pallas_call(
        _ple_kernel,
        grid=(t // tm, n // tn),
        in_specs=[pl.BlockSpec((tm, d), lambda i, j: (i, 0)),
                  pl.BlockSpec((tm, pd), lambda i, j: (i, 0)),
                  pl.BlockSpec((d, tn), lambda i, j: (0, j)),
                  pl.BlockSpec((pd, tn), lambda i, j: (0, j))],
        out_specs=pl.BlockSpec((tm, tn), lambda i, j: (i, j)),
        out_shape=jax.ShapeDtypeStruct((t, n), F32),
        compiler_params=_cparams(("parallel", "parallel")),
    )(xb, p, wpg, wpp)


def _router_kernel(x_ref, w_ref, o_ref, *, tm):
    logits = jnp.dot(x_ref[...], w_ref[...], precision=HIGHEST, preferred_element_type=F32)
    lane = lax.broadcasted_iota(jnp.int32, (tm, LANES), 1)
    neg = -jnp.inf
    l1 = jnp.where(lane < N_EXPERTS, logits, neg)
    m1 = jnp.max(l1, -1, keepdims=True)
    i1 = jnp.min(jnp.where(l1 == m1, lane, LANES), -1, keepdims=True)
    l2 = jnp.where(lane == i1, neg, l1)
    m2 = jnp.max(l2, -1, keepdims=True)
    i2 = jnp.min(jnp.where(l2 == m2, lane, LANES), -1, keepdims=True)
    e = jnp.exp(m2 - m1)
    w1 = 1.0 / (1.0 + e)
    w2 = e / (1.0 + e)
    o_ref[...] = jnp.where(lane == 0, i1.astype(F32),
                           jnp.where(lane == 1, i2.astype(F32),
                                     jnp.where(lane == 2, w1, jnp.where(lane == 3, w2, 0.0))))


def router(xf, rw, *, tm=512):
    t, d = xf.shape
    tm = min(tm, t)
    return pl.pallas_call(
        functools.partial(_router_kernel, tm=tm),
        grid=(t // tm,),
        in_specs=[pl.BlockSpec((tm, d), lambda i: (i, 0)),
                  pl.BlockSpec((d, LANES), lambda i: (0, 0))],
        out_specs=pl.BlockSpec((tm, LANES), lambda i: (i, 0)),
        out_shape=jax.ShapeDtypeStruct((t, LANES), F32),
        compiler_params=_cparams(("parallel",)),
    )(xf, rw)


def _row_copy(src_hbm, dst_vmem, sem, src_row, dst_row):
    return pltpu.make_async_copy(src_hbm.at[pl.ds(src_row, 1)], dst_vmem.at[pl.ds(dst_row, 1)], sem)


def _gather_kernel(tok_ref, x_hbm, o_ref, buf, sem, *, tg):
    base = pl.program_id(0) * tg

    def issue(r, c):
        _row_copy(x_hbm, buf, sem, tok_ref[base + r], r).start()
        return c

    lax.fori_loop(0, tg, issue, 0)

    def drain(r, c):
        _row_copy(x_hbm, buf, sem, 0, r).wait()
        return c

    lax.fori_loop(0, tg, drain, 0)
    o_ref[...] = buf[...].astype(o_ref.dtype)


def gather_rows(xf, tok, *, tg=256):
    p = tok.shape[0]
    d = xf.shape[1]
    return pl.pallas_call(
        functools.partial(_gather_kernel, tg=tg),
        grid_spec=pltpu.PrefetchScalarGridSpec(
            num_scalar_prefetch=1,
            grid=(p // tg,),
            in_specs=[pl.BlockSpec(memory_space=pl.ANY)],
            out_specs=pl.BlockSpec((tg, d), lambda i, tok: (i, 0)),
            scratch_shapes=[pltpu.VMEM((tg, d), F32), pltpu.SemaphoreType.DMA(())]),
        out_shape=jax.ShapeDtypeStruct((p, d), BF16),
        compiler_params=_cparams(("arbitrary",)),
    )(tok, xf)


def _gswiglu_kernel(te_ref, x_ref, wg_ref, wu_ref, o_ref, *, ntile):
    i = pl.program_id(1)

    @pl.when(i < te_ref[ntile])
    def _():
        x = x_ref[...]
        o_ref[...] = (_silu(_dot(x, wg_ref[...])) * _dot(x, wu_ref[...])).astype(o_ref.dtype)

    @pl.when(i >= te_ref[ntile])
    def _():
        o_ref[...] = jnp.zeros_like(o_ref)


def grouped_swiglu_up(xs, wgu, te, *, tm, tn=512):
    p, d = xs.shape
    f = wgu.shape[2] // 2
    nf = f // tn
    ntile = p // tm
    return pl.pallas_call(
        functools.partial(_gswiglu_kernel, ntile=ntile),
        grid_spec=pltpu.PrefetchScalarGridSpec(
            num_scalar_prefetch=1,
            grid=(nf, ntile),
            in_specs=[pl.BlockSpec((tm, d), lambda j, i, te: (i, 0)),
                      pl.BlockSpec((None, d, tn), lambda j, i, te: (te[i], 0, j)),
                      pl.BlockSpec((None, d, tn), lambda j, i, te: (te[i], 0, nf + j))],
            out_specs=pl.BlockSpec((tm, tn), lambda j, i, te: (i, j))),
        out_shape=jax.ShapeDtypeStruct((p, f), BF16),
        compiler_params=_cparams(("parallel", "parallel")),
    )(te, xs, wgu, wgu)


def _gdown_kernel(te_ref, h_ref, w_ref, o_ref, acc_ref, *, ntile, nk):
    i = pl.program_id(0)
    k = pl.program_id(1)

    @pl.when(k == 0)
    def _():
        acc_ref[...] = jnp.zeros_like(acc_ref)

    @pl.when(i < te_ref[ntile])
    def _():
        acc_ref[...] += _dot(h_ref[...], w_ref[...])

    @pl.when(k == nk - 1)
    def _():
        o_ref[...] = acc_ref[...].astype(o_ref.dtype)


def grouped_down(h, wd, te, *, tm, tk=1024):
    p, f = h.shape
    d = wd.shape[2]
    ntile = p // tm
    nk = f // tk
    return pl.pallas_call(
        functools.partial(_gdown_kernel, ntile=ntile, nk=nk),
        grid_spec=pltpu.PrefetchScalarGridSpec(
            num_scalar_prefetch=1,
            grid=(ntile, nk),
            in_specs=[pl.BlockSpec((tm, tk), lambda i, k, te: (i, k)),
                      pl.BlockSpec((None, tk, d), lambda i, k, te: (te[i], k, 0))],
            out_specs=pl.BlockSpec((tm, d), lambda i, k, te: (i, 0)),
            scratch_shapes=[pltpu.VMEM((tm, d), F32)]),
        out_shape=jax.ShapeDtypeStruct((p, d), F32),
        compiler_params=_cparams(("parallel", "arbitrary")),
    )(te, h, wd)


def _combine_ln_kernel(s1_ref, s2_ref, y_hbm, x_ref, ple_ref, rt_ref, g_ref, b_ref, of_ref, ob_ref,
                       buf1, buf2, sem, *, tc, alpha):
    base = pl.program_id(0) * tc

    def issue(r, c):
        _row_copy(y_hbm, buf1, sem, s1_ref[base + r], r).start()
        _row_copy(y_hbm, buf2, sem, s2_ref[base + r], r).start()
        return c

    lax.fori_loop(0, tc, issue, 0)

    def drain(r, c):
        _row_copy(y_hbm, buf1, sem, 0, r).wait()
        _row_copy(y_hbm, buf2, sem, 0, r).wait()
        return c

    lax.fori_loop(0, tc, drain, 0)
    rt = rt_ref[...]
    ff = rt[:, 2:3] * buf1[...] + rt[:, 3:4] * buf2[...]
    y = _ln_rows(alpha * x_ref[...] + ff + ple_ref[...], g_ref[...], b_ref[...])
    of_ref[...] = y
    ob_ref[...] = y.astype(BF16)


def combine_ln(ys, s1, s2, x, ple, rt, g, b, *, alpha, tc=256):
    t, d = x.shape
    tc = min(tc, t)
    row = pl.BlockSpec((tc, d), lambda i, a, c: (i, 0))
    vec = pl.BlockSpec((1, d), lambda i, a, c: (0, 0))
    return pl.pallas_call(
        functools.partial(_combine_ln_kernel, tc=tc, alpha=alpha),
        grid_spec=pltpu.PrefetchScalarGridSpec(
            num_scalar_prefetch=2,
            grid=(t // tc,),
            in_specs=[pl.BlockSpec(memory_space=pl.ANY), row, row,
                      pl.BlockSpec((tc, LANES), lambda i, a, c: (i, 0)), vec, vec],
            out_specs=[row, row],
            scratch_shapes=[pltpu.VMEM((tc, d), F32), pltpu.VMEM((tc, d), F32),
                            pltpu.SemaphoreType.DMA(())]),
        out_shape=[jax.ShapeDtypeStruct((t, d), F32), jax.ShapeDtypeStruct((t, d), BF16)],
        compiler_params=_cparams(("arbitrary",)),
    )(s1, s2, ys, x, ple, rt, g.reshape(1, d), b.reshape(1, d))


def _routing_tables(rt, *, tm):
    t = rt.shape[0]
    e_flat = rt[:, :TOP_K].astype(jnp.int32).reshape(-1)
    na = e_flat.shape[0]
    p = na + N_EXPERTS * tm
    ntile = p // tm
    onehot = (e_flat[:, None] == jnp.arange(N_EXPERTS, dtype=jnp.int32)[None, :]).astype(jnp.int32)
    csum = jnp.cumsum(onehot, axis=0)
    counts = csum[-1]
    rank = jnp.sum((csum - onehot) * onehot, axis=1)
    padded = ((counts + tm - 1) // tm) * tm
    ends = jnp.cumsum(padded)
    starts = ends - padded
    slot = starts[e_flat] + rank
    tok = jnp.zeros((p,), jnp.int32).at[slot].set(jnp.arange(na, dtype=jnp.int32) // TOP_K)
    tile_start = jnp.arange(ntile, dtype=jnp.int32) * tm
    te = jnp.minimum(jnp.sum((tile_start[:, None] >= ends[None, :]).astype(jnp.int32), axis=1),
                     N_EXPERTS - 1)
    te = jnp.concatenate([te, (ends[-1] // tm)[None]]).astype(jnp.int32)
    slot2 = slot.reshape(t, TOP_K)
    return tok, te, slot2[:, 0], slot2[:, 1]


def kernel(x, p, emb_ln_g, emb_ln_b, w_in, conv_w, a_log, dt_bias, o_norm_w, w_fourier, w_delta, w_out,
           ln1_g, ln1_b, ffn_gate_up, ffn_down, router_w, exp_gate_up, exp_down, ple_gate, ple_proj,
           ln2_g, ln2_b):
    batch, seq, d = x.shape
    depth = w_in.shape[0]
    t = batch * seq
    alpha = float((2 * depth) ** 0.25)
    nchunk = seq // CHUNK
    main_w = FOURIER_WIDTH + 4 * DN_WIDTH
    small0 = main_w
    gate0 = main_w + 4 * DN_HEADS
    moe_tm = 512

    chan_t, seq_t = _dft_tables(seq)
    xf, xb = layer_norm(x.reshape(t, d), emb_ln_g, emb_ln_b)
    pad_lanes = lambda v: jnp.pad(v, ((0, 0), (0, LANES - v.shape[1])))

    for i in range(depth):
        w_main = w_in[i, :, :main_w].astype(BF16)
        w_gate = w_in[i, :, gate0:].astype(BF16)
        w_beta = pad_lanes(w_in[i, :, small0:small0 + 2 * DN_HEADS])
        w_a = pad_lanes(w_in[i, :, small0 + 2 * DN_HEADS:gate0])

        proj = matmul(xb, w_main, out_dtype=F32, tm=1024, tn=1024)
        gates_fd = matmul(xb, w_gate, out_dtype=F32, tm=1024, tn=1024)
        beta, gc = dn_gates(xf, w_beta, w_a, pad_lanes(a_log[i].reshape(1, -1)),
                            pad_lanes(dt_bias[i].reshape(1, -1)))
        gct = gc[:, :2 * DN_HEADS].reshape(batch, seq, 2 * DN_HEADS).transpose(0, 2, 1)
        gct = gct.reshape(batch, 2 * DN_HEADS, nchunk, CHUNK)

        vch = chan_dft(proj, chan_t, batch=batch, seq=seq)
        a_f = matmul(seq_t, vch.reshape(batch, 2 * seq, FOURIER_WIDTH), out_dtype=BF16,
                     tm=1024, tn=1024, tk=2048).reshape(t, FOURIER_WIDTH)

        qkvn = dn_prep(proj, conv_w[i], batch=batch, seq=seq, col0=FOURIER_WIDTH)
        a_d = dn_core(qkvn, proj, beta, gc, gct, o_norm_w[i].reshape(1, -1), batch=batch, seq=seq,
                      z_col0=FOURIER_WIDTH + 3 * DN_WIDTH)

        merged = merge_branches(a_f, a_d, w_fourier[i].astype(BF16), w_delta[i].astype(BF16), gates_fd)
        xf, xb = mm_res_ln(merged, w_out[i].astype(BF16), xf, ln1_g[i], ln1_b[i], alpha=alpha)

        ple = ple_term(xb, p[i].reshape(t, -1), ple_gate[i].astype(BF16), ple_proj[i].astype(BF16))
        if i % 2 == 0:
            hmid = swiglu_up(xb, ffn_gate_up[i // 2].astype(BF16))
            xf, xb = mm_res_ln(hmid, ffn_down[i // 2].astype(BF16), xf, ln2_g[i], ln2_b[i],
                               alpha=alpha, extra=ple)
        else:
            j = i // 2
            rt = router(xf, pad_lanes(router_w[j]))
            tok, te, s1, s2 = _routing_tables(rt, tm=moe_tm)
            xs = gather_rows(xf, tok)
            hmid = grouped_swiglu_up(xs, exp_gate_up[j].astype(BF16), te, tm=moe_tm)
            ys = grouped_down(hmid, exp_down[j].astype(BF16), te, tm=moe_tm)
            xf, xb = combine_ln(ys, s1, s2, xf, ple, rt, ln2_g[i], ln2_b[i], alpha=alpha)
    return xf.reshape(batch, seq, d)
```

```python
import functools
import math

import numpy as np
import jax
import jax.numpy as jnp
from jax import lax
from jax.experimental import pallas as pl
from jax.experimental.pallas import tpu as pltpu

F32 = jnp.float32
BF16 = jnp.bfloat16
HIGHEST = lax.Precision.HIGHEST

FOURIER_GROUPS = 4
FOURIER_GROUP_DIM = 256
FOURIER_WIDTH = FOURIER_GROUPS * FOURIER_GROUP_DIM
DN_HEADS = 16
DN_HEAD_DIM = 128
DN_WIDTH = DN_HEADS * DN_HEAD_DIM
CONV_K = 5
CHUNK = 64
N_EXPERTS = 8
TOP_K = 2
LN_EPS = 1e-5
RMS_EPS = 1e-6
L2_EPS = 1e-6
LANES = 128
VMEM_LIMIT = 56 * 1024 * 1024


def _cparams(sem):
    return pltpu.CompilerParams(dimension_semantics=sem, vmem_limit_bytes=VMEM_LIMIT)


def _div(a, b):
    assert a % b == 0, (a, b)
    return a // b


def _tile(dim, target):
    if dim <= target:
        return dim
    cand = (target // LANES) * LANES
    while dim % cand:
        cand -= LANES
    return cand


def _dot(a, b):
    return jnp.dot(a, b, preferred_element_type=F32)


def _dot_nt(a, b):
    return lax.dot_general(a, b, (((1,), (1,)), ((), ())), preferred_element_type=F32)


def _dot_tn(a, b):
    return lax.dot_general(a, b, (((0,), (0,)), ((), ())), preferred_element_type=F32)


def _sigmoid(v):
    return 1.0 / (1.0 + jnp.exp(-v))


def _silu(v):
    return v * _sigmoid(v)


def _ln_rows(v, g, b):
    mu = jnp.mean(v, -1, keepdims=True)
    d = v - mu
    var = jnp.mean(d * d, -1, keepdims=True)
    return d * lax.rsqrt(var + LN_EPS) * g + b


def _ln_kernel(x_ref, g_ref, b_ref, of_ref, ob_ref):
    y = _ln_rows(x_ref[...], g_ref[...], b_ref[...])
    of_ref[...] = y
    ob_ref[...] = y.astype(BF16)


def layer_norm(x, g, b, *, tm=512):
    t, d = x.shape
    tm = min(tm, t)
    row = pl.BlockSpec((tm, d), lambda i: (i, 0))
    vec = pl.BlockSpec((1, d), lambda i: (0, 0))
    return pl.pallas_call(
        _ln_kernel,
        name="ln_embed",
        grid=(_div(t, tm),),
        in_specs=[row, vec, vec],
        out_specs=[row, row],
        out_shape=[jax.ShapeDtypeStruct((t, d), F32), jax.ShapeDtypeStruct((t, d), BF16)],
        compiler_params=_cparams(("parallel",)),
    )(x, g.reshape(1, d), b.reshape(1, d))


def _mm_kernel(a_ref, w_ref, o_ref, *scratch, nk):
    if nk == 1:
        o_ref[...] = _dot(a_ref[...].astype(BF16), w_ref[...]).astype(o_ref.dtype)
        return
    acc_ref, = scratch
    k = pl.program_id(3)

    @pl.when(k == 0)
    def _():
        acc_ref[...] = jnp.zeros_like(acc_ref)

    acc_ref[...] += _dot(a_ref[...].astype(BF16), w_ref[...])

    @pl.when(k == nk - 1)
    def _():
        o_ref[...] = acc_ref[...].astype(o_ref.dtype)


def matmul(a, w, *, out_dtype, tm, tn, tk=None, a_k0=0, name="matmul"):
    batched = w.ndim == 3
    if not batched:
        w = w[None]
    nb, kdim, n = w.shape
    m = a.shape[0]
    tm, tn = min(tm, m), min(tn, n)
    tk = kdim if tk is None else _tile(kdim, tk)
    nk = _div(kdim, tk)
    k0 = _div(a_k0, tk)
    out = pl.pallas_call(
        functools.partial(_mm_kernel, nk=nk),
        name=name,
        grid=(nb, _div(m, tm), _div(n, tn), nk),
        in_specs=[pl.BlockSpec((tm, tk), lambda b, i, j, k: (i, k0 + k)),
                  pl.BlockSpec((None, tk, tn), lambda b, i, j, k: (b, k, j))],
        out_specs=pl.BlockSpec((None, tm, tn), lambda b, i, j, k: (b, i, j)),
        out_shape=jax.ShapeDtypeStruct((nb, m, n), out_dtype),
        scratch_shapes=[] if nk == 1 else [pltpu.VMEM((tm, tn), F32)],
        compiler_params=_cparams(("parallel", "parallel", "parallel", "arbitrary")),
    )(a, w)
    return out if batched else out[0]


def _gates_kernel(x_ref, wb_ref, wa_ref, alog_ref, dtb_ref, beta_ref, gc_ref, *, tm):
    x = x_ref[...]
    braw = jnp.dot(x, wb_ref[...], precision=HIGHEST, preferred_element_type=F32)
    araw = jnp.dot(x, wa_ref[...], precision=HIGHEST, preferred_element_type=F32)
    beta_ref[...] = _sigmoid(braw)
    zz = araw + dtb_ref[...]
    softplus = jnp.maximum(zz, 0.0) + jnp.log(1.0 + jnp.exp(-jnp.abs(zz)))
    g = -jnp.exp(alog_ref[...]) * softplus
    pos = lax.broadcasted_iota(jnp.int32, (tm, LANES), 0) % CHUNK
    lane = lax.broadcasted_iota(jnp.int32, (tm, LANES), 1)
    pre = g
    suf = g
    s = 1
    while s < CHUNK:
        pre = pre + jnp.where(pos >= s, pltpu.roll(pre, s, 0), 0.0)
        suf = suf + jnp.where(pos < CHUNK - s, pltpu.roll(suf, tm - s, 0), 0.0)
        s *= 2
    gc_ref[...] = jnp.where(lane < DN_HEADS, pre, suf)


def dn_gates(xf, wb, wa, alog, dtb, *, tm=512):
    t, d = xf.shape
    tm = min(tm, t)
    row = pl.BlockSpec((tm, d), lambda i: (i, 0))
    wsp = pl.BlockSpec((d, LANES), lambda i: (0, 0))
    vec = pl.BlockSpec((1, LANES), lambda i: (0, 0))
    osp = pl.BlockSpec((tm, LANES), lambda i: (i, 0))
    return pl.pallas_call(
        functools.partial(_gates_kernel, tm=tm),
        name="dn_gates",
        grid=(_div(t, tm),),
        in_specs=[row, wsp, wsp, vec, vec],
        out_specs=[osp, osp],
        out_shape=[jax.ShapeDtypeStruct((t, LANES), F32)] * 2,
        compiler_params=_cparams(("parallel",)),
    )(xf, wb, wa, alog, dtb)


_CONV_PAD = 8
_CONV_ROWS = 256


def _dn_prep_kernel(u_ref, w_ref, o_ref, pad_ref, *, seq):
    j = pl.program_id(1)
    half = (CONV_K - 1) // 2
    zeros = jnp.zeros((_CONV_PAD, LANES), F32)
    pad_ref[0:_CONV_PAD, :] = zeros
    pad_ref[_CONV_PAD + seq:_CONV_PAD + seq + _CONV_PAD, :] = zeros
    pad_ref[_CONV_PAD:_CONV_PAD + seq, :] = u_ref[...]
    w = w_ref[...]
    is_qk = j < 2 * DN_HEADS
    scale = jnp.where(j < DN_HEADS, DN_HEAD_DIM ** -0.5, 1.0).astype(F32)
    rows = min(_CONV_ROWS, seq)
    for r0 in range(0, seq, rows):
        acc = jnp.zeros((rows, LANES), F32)
        for t in range(CONV_K):
            lo = _CONV_PAD + r0 + t - half
            acc = acc + pad_ref[lo:lo + rows, :] * w[t:t + 1, :]
        y = _silu(acc)
        nrm = lax.rsqrt(jnp.sum(y * y, -1, keepdims=True) + L2_EPS) * scale
        o_ref[r0:r0 + rows, :] = (y * jnp.where(is_qk, nrm, 1.0)).astype(o_ref.dtype)


def dn_prep(proj, conv_w, *, batch, seq, col0):
    nblk = 3 * DN_HEADS
    c0 = col0 // LANES
    return pl.pallas_call(
        functools.partial(_dn_prep_kernel, seq=seq),
        name="dn_prep",
        grid=(batch, nblk),
        in_specs=[pl.BlockSpec((seq, LANES), lambda b, j: (b, c0 + j)),
                  pl.BlockSpec((CONV_K, LANES), lambda b, j: (0, j))],
        out_specs=pl.BlockSpec((seq, LANES), lambda b, j: (b, j)),
        out_shape=jax.ShapeDtypeStruct((batch * seq, 3 * DN_WIDTH), BF16),
        scratch_shapes=[pltpu.VMEM((seq + 2 * _CONV_PAD, LANES), F32)],
        compiler_params=_cparams(("parallel", "parallel")),
    )(proj, conv_w)


_DN_UNROLL_LOCAL = 8
_DN_UNROLL_OUT = 8


def _dn_core_kernel(q_ref, k_ref, v_ref, z_ref, beta_ref, gc_ref, gtf_ref, gtb_ref, onw_ref, o_ref,
                    wq_s, cm_s, u_s, bm_s, qk_s, gl_s, st_s, *, nchunk, unroll, unroll_out):
    h = pl.program_id(1)
    hd = DN_HEAD_DIM
    c2 = 2 * CHUNK
    lane = lax.broadcasted_iota(jnp.int32, (CHUNK, LANES), 1)
    ri = lax.broadcasted_iota(jnp.int32, (CHUNK, CHUNK), 0)
    ci = lax.broadcasted_iota(jnp.int32, (CHUNK, CHUNK), 1)
    gt_refs = (gtf_ref, gtb_ref)
    causal = (ri >= ci, ri <= ci)
    strict = (ri > ci, ri < ci)

    def local(gi, carry):
        chunks = []
        for j in range(unroll):
            c = gi * unroll + j
            r0 = pl.multiple_of(c * CHUNK, CHUNK)
            kb = k_ref[pl.ds(r0, CHUNK), :]
            qb = q_ref[pl.ds(r0, CHUNK), :]
            kq = _dot_nt(jnp.concatenate([kb, qb], axis=0), kb)
            chunks.append(dict(c=c, r0=r0, kb=kb, qb=qb, kk=kq[:CHUNK], qk=kq[CHUNK:]))
        chains = []
        for cd in chunks:
            c, r0 = cd["c"], cd["r0"]
            bblk = beta_ref[pl.ds(r0, CHUNK), :]
            gblk = gc_ref[pl.ds(r0, CHUNK), :]
            for d in (0, 1):
                sel = lane == (d * DN_HEADS + h)
                beta_col = jnp.sum(jnp.where(sel, bblk, 0.0), -1, keepdims=True)
                gc_col = jnp.sum(jnp.where(sel, gblk, 0.0), -1, keepdims=True)
                gc_row = gt_refs[d][pl.ds(c, 1), :]
                dec = jnp.exp(jnp.where(causal[d], gc_col - gc_row, -jnp.inf))
                mp = jnp.where(strict[d], -(beta_col * cd["kk"] * dec), 0.0)
                qk_s[d, pl.ds(r0, CHUNK), :] = jnp.where(causal[d], cd["qk"] * dec, 0.0).astype(BF16)
                chains.append(dict(c=c, d=d, r0=r0, beta=beta_col, gc_col=gc_col, gc_row=gc_row,
                                   mp=mp, nm=mp))
        for ch in chains:
            mpb = ch["mp"].astype(BF16)
            ch["mp"] = _dot(mpb, mpb)
        for _ in range(4):
            for ch in chains:
                mpb = ch["mp"].astype(BF16)
                r = _dot(jnp.concatenate([mpb, ch["nm"].astype(BF16)], axis=0), mpb)
                ch["nm"] = ch["nm"] + ch["mp"] + r[CHUNK:]
                ch["mp"] = r[:CHUNK]
        for ch in chains:
            ch["nm"] = ch["nm"] + ch["mp"] + _dot(ch["nm"].astype(BF16), ch["mp"].astype(BF16))
        for ch in chains:
            c, d, r0 = ch["c"], ch["d"], ch["r0"]
            beta_col, gc_col, gc_row = ch["beta"], ch["gc_col"], ch["gc_row"]
            kf = k_ref[pl.ds(r0, CHUNK), :].astype(F32)
            vf = v_ref[pl.ds(r0, CHUNK), :].astype(F32)
            ecol = jnp.exp(gc_col)
            rhs = jnp.concatenate([vf * beta_col, kf * (beta_col * ecol)], axis=1)
            ch["sol"] = rhs + _dot(ch["nm"].astype(BF16), rhs.astype(BF16))
            gc_last = gc_row[:, CHUNK - 1:CHUNK] if d == 0 else gc_row[:, 0:1]
            ch["kdb"] = (kf * jnp.exp(gc_last - gc_col)).astype(BF16)
            m0 = pl.multiple_of(c * hd, hd)
            wq_s[d, pl.ds(m0 + CHUNK, CHUNK), :] = (q_ref[pl.ds(r0, CHUNK), :].astype(F32) * ecol).astype(BF16)
            gl_s[d, pl.ds(c, 1), :] = jnp.broadcast_to(jnp.exp(gc_last), (1, LANES))
        for ch in chains:
            c, d, r0 = ch["c"], ch["d"], ch["r0"]
            solb = ch["sol"].astype(BF16)
            kuw = _dot_tn(ch["kdb"], solb)
            m0 = pl.multiple_of(c * hd, hd)
            wq_s[d, pl.ds(m0, CHUNK), :] = solb[:, hd:]
            cm_s[d, pl.ds(m0, hd), :] = (-kuw[:, hd:]).astype(BF16)
            u_s[d, pl.ds(r0, CHUNK), :] = ch["sol"][:, :hd]
            bm_s[d, pl.ds(m0, hd), :] = kuw[:, :hd]
        return carry

    lax.fori_loop(0, nchunk // unroll, local, 0)

    def scan_step(d, c, state):
        m0 = pl.multiple_of(c * hd, hd)
        sb = state.astype(BF16)
        st_s[d, pl.ds(m0, hd), :] = sb
        return (state * gl_s[d, pl.ds(c, 1), :] + _dot(cm_s[d, pl.ds(m0, hd), :], sb)
                + bm_s[d, pl.ds(m0, hd), :])

    def scan(i, carry):
        sf, sb = carry
        return scan_step(0, i, sf), scan_step(1, nchunk - 1 - i, sb)

    s0 = jnp.zeros((hd, hd), F32)
    lax.fori_loop(0, nchunk, scan, (s0, s0))

    onw = onw_ref[...]

    def outputs(gi, carry):
        rows = [pl.multiple_of((gi * unroll_out + j) * CHUNK, CHUNK) for j in range(unroll_out)]
        mats = [pl.multiple_of((gi * unroll_out + j) * hd, hd) for j in range(unroll_out)]
        r = [[_dot(wq_s[d, pl.ds(m0, c2), :], st_s[d, pl.ds(m0, hd), :])
              for d in (0, 1)] for m0 in mats]
        o = []
        for j, r0 in enumerate(rows):
            od = []
            for d in (0, 1):
                vnew = (u_s[d, pl.ds(r0, CHUNK), :] - r[j][d][:CHUNK]).astype(BF16)
                od.append(r[j][d][CHUNK:] + _dot(qk_s[d, pl.ds(r0, CHUNK), :], vnew))
            o.append(od[0] + od[1])
        for j, r0 in enumerate(rows):
            zz = z_ref[pl.ds(r0, CHUNK), :]
            y = o[j] * lax.rsqrt(jnp.mean(o[j] * o[j], -1, keepdims=True) + RMS_EPS) * onw * _silu(zz)
            o_ref[pl.ds(r0, CHUNK), :] = y.astype(o_ref.dtype)
        return carry

    lax.fori_loop(0, nchunk // unroll_out, outputs, 0)


def dn_core(qkvn, proj, beta, gc, gct, onw, *, batch, seq, z_col0):
    nchunk = _div(seq, CHUNK)
    unroll = math.gcd(_DN_UNROLL_LOCAL, nchunk)
    unroll_out = math.gcd(_DN_UNROLL_OUT, nchunk)
    zc = z_col0 // LANES
    hd = DN_HEAD_DIM
    blk = lambda off: pl.BlockSpec((seq, LANES), lambda b, h: (b, off + h))
    gsp = pl.BlockSpec((seq, LANES), lambda b, h: (b, 0))
    return pl.pallas_call(
        functools.partial(_dn_core_kernel, nchunk=nchunk, unroll=unroll, unroll_out=unroll_out),
        name="dn_core",
        grid=(batch, DN_HEADS),
        in_specs=[blk(0), blk(DN_HEADS), blk(2 * DN_HEADS), blk(zc), gsp, gsp,
                  pl.BlockSpec((None, None, nchunk, CHUNK), lambda b, h: (b, h, 0, 0)),
                  pl.BlockSpec((None, None, nchunk, CHUNK), lambda b, h: (b, DN_HEADS + h, 0, 0)),
                  pl.BlockSpec((1, LANES), lambda b, h: (0, 0))],
        out_specs=pl.BlockSpec((seq, LANES), lambda b, h: (b, h)),
        out_shape=jax.ShapeDtypeStruct((batch * seq, DN_WIDTH), BF16),
        scratch_shapes=[pltpu.VMEM((2, 2 * seq, LANES), BF16),
                        pltpu.VMEM((2, nchunk * hd, LANES), BF16),
                        pltpu.VMEM((2, seq, LANES), F32),
                        pltpu.VMEM((2, nchunk * hd, LANES), F32),
                        pltpu.VMEM((2, seq, CHUNK), BF16),
                        pltpu.VMEM((2, nchunk, LANES), F32),
                        pltpu.VMEM((2, nchunk * hd, LANES), BF16)],
        compiler_params=_cparams(("parallel", "parallel")),
    )(qkvn, qkvn, qkvn, proj, beta, gc, gct, gct, onw)


def _chan_dft_kernel(u_ref, t_ref, o_ref):
    r = _dot(u_ref[...].astype(BF16), t_ref[...])
    g = FOURIER_GROUP_DIM
    o_ref[0] = r[:, :g].astype(BF16)
    o_ref[1] = r[:, g:].astype(BF16)


def chan_dft(proj, table, *, batch, seq, tm=1024):
    tm = min(tm, seq)
    nsb = _div(seq, tm)
    g = FOURIER_GROUP_DIM
    return pl.pallas_call(
        _chan_dft_kernel,
        name="chan_dft",
        grid=(batch, nsb, FOURIER_GROUPS),
        in_specs=[pl.BlockSpec((tm, g), lambda b, i, j: (b * nsb + i, j)),
                  pl.BlockSpec((g, 2 * g), lambda b, i, j: (0, 0))],
        out_specs=pl.BlockSpec((None, 2, tm, g), lambda b, i, j: (b, 0, i, j)),
        out_shape=jax.ShapeDtypeStruct((batch, 2, seq, FOURIER_WIDTH), BF16),
        compiler_params=_cparams(("parallel", "parallel", "parallel")),
    )(proj, table)


def _dft_tables(seq):
    g = FOURIER_GROUP_DIM
    jc = np.arange(g)
    ang_c = 2.0 * np.pi * ((jc[:, None] * jc[None, :]) % g) / g
    sc = 1.0 / math.sqrt(g)
    chan = np.concatenate([np.cos(ang_c) * sc, -np.sin(ang_c) * sc], axis=1)
    r = 1 << ((seq.bit_length() - 1) // 2)
    assert seq & (seq - 1) == 0 and seq % r == 0, seq
    j = np.arange(seq)[:, None]
    a1 = 2.0 * np.pi * ((j * r * np.arange(seq // r)[None, :]) % seq) / seq
    a0 = 2.0 * np.pi * ((j * np.arange(r)[None, :]) % seq) / seq
    ss = 1.0 / math.sqrt(seq)
    c1, s1 = jnp.asarray(np.cos(a1) * ss, F32)[:, :, None], jnp.asarray(np.sin(a1) * ss, F32)[:, :, None]
    c0, s0 = jnp.asarray(np.cos(a0), F32)[:, None, :], jnp.asarray(np.sin(a0), F32)[:, None, :]
    cos_t = (c1 * c0 - s1 * s0).reshape(seq, seq)
    sin_t = (s1 * c0 + c1 * s0).reshape(seq, seq)
    seq_t = jnp.concatenate([cos_t, sin_t], axis=1).astype(BF16)
    return jnp.asarray(chan, dtype=BF16), seq_t


def _merge_kernel(af_ref, ad_ref, wf_ref, wd_ref, gf_ref, gd_ref, o_ref):
    yf = _dot(af_ref[...], wf_ref[...])
    yd = _dot(ad_ref[...], wd_ref[...])
    o_ref[...] = (_sigmoid(gf_ref[...]) * yf + _sigmoid(gd_ref[...]) * yd).astype(o_ref.dtype)


def merge_branches(af, ad, wf, wd, gates, *, tm=512, tn=512):
    t, kf = af.shape
    kd = ad.shape[1]
    n = wf.shape[1]
    tm, tn = min(tm, t), min(tn, n)
    nb = _div(n, tn)
    return pl.pallas_call(
        _merge_kernel,
        name="merge",
        grid=(_div(t, tm), nb),
        in_specs=[pl.BlockSpec((tm, kf), lambda i, j: (i, 0)),
                  pl.BlockSpec((tm, kd), lambda i, j: (i, 0)),
                  pl.BlockSpec((kf, tn), lambda i, j: (0, j)),
                  pl.BlockSpec((kd, tn), lambda i, j: (0, j)),
                  pl.BlockSpec((tm, tn), lambda i, j: (i, j)),
                  pl.BlockSpec((tm, tn), lambda i, j: (i, nb + j))],
        out_specs=pl.BlockSpec((tm, tn), lambda i, j: (i, j)),
        out_shape=jax.ShapeDtypeStruct((t, n), BF16),
        compiler_params=_cparams(("parallel", "parallel")),
    )(af, ad, wf, wd, gates, gates)


def _mm_res_ln_kernel(*refs, nk, alpha, has_extra):
    if has_extra:
        a_ref, w_ref, x_ref, e_ref, g_ref, b_ref, of_ref, ob_ref, acc_ref = refs
    else:
        a_ref, w_ref, x_ref, g_ref, b_ref, of_ref, ob_ref, acc_ref = refs
        e_ref = None
    k = pl.program_id(1)

    @pl.when(k == 0)
    def _():
        acc_ref[...] = jnp.zeros_like(acc_ref)

    acc_ref[...] += _dot(a_ref[...], w_ref[...])

    @pl.when(k == nk - 1)
    def _():
        v = alpha * x_ref[...] + acc_ref[...]
        if has_extra:
            v = v + e_ref[...]
        y = _ln_rows(v, g_ref[...], b_ref[...])
        of_ref[...] = y
        ob_ref[...] = y.astype(BF16)


def mm_res_ln(a, w, x, g, b, *, alpha, extra=None, tm=512, tk=1024):
    t, kdim = a.shape
    d = w.shape[1]
    tm, tk = min(tm, t), _tile(kdim, tk)
    nk = _div(kdim, tk)
    row = pl.BlockSpec((tm, d), lambda i, k: (i, 0))
    vec = pl.BlockSpec((1, d), lambda i, k: (0, 0))
    in_specs = [pl.BlockSpec((tm, tk), lambda i, k: (i, k)),
                pl.BlockSpec((tk, d), lambda i, k: (k, 0)), row]
    args = [a, w, x]
    if extra is not None:
        in_specs.append(row)
        args.append(extra)
    in_specs += [vec, vec]
    args += [g.reshape(1, d), b.reshape(1, d)]
    return pl.pallas_call(
        functools.partial(_mm_res_ln_kernel, nk=nk, alpha=alpha, has_extra=extra is not None),
        name="mm_res_ln",
        grid=(_div(t, tm), nk),
        in_specs=in_specs,
        out_specs=[row, row],
        out_shape=[jax.ShapeDtypeStruct((t, d), F32), jax.ShapeDtypeStruct((t, d), BF16)],
        scratch_shapes=[pltpu.VMEM((tm, d), F32)],
        compiler_params=_cparams(("parallel", "arbitrary")),
    )(*args)


def _swiglu_kernel(x_ref, wg_ref, wu_ref, o_ref):
    x = x_ref[...]
    o_ref[...] = (_silu(_dot(x, wg_ref[...])) * _dot(x, wu_ref[...])).astype(o_ref.dtype)


def swiglu_up(x, wgu, *, tm=1024, tn=512):
    t, d = x.shape
    f = wgu.shape[1] // 2
    tm, tn = min(tm, t), min(tn, f)
    nf = _div(f, tn)
    return pl.pallas_call(
        _swiglu_kernel,
        name="swiglu_up",
        grid=(_div(t, tm), nf),
        in_specs=[pl.BlockSpec((tm, d), lambda i, j: (i, 0)),
                  pl.BlockSpec((d, tn), lambda i, j: (0, j)),
                  pl.BlockSpec((d, tn), lambda i, j: (0, nf + j))],
        out_specs=pl.BlockSpec((tm, tn), lambda i, j: (i, j)),
        out_shape=jax.ShapeDtypeStruct((t, f), BF16),
        compiler_params=_cparams(("parallel", "parallel")),
    )(x, wgu, wgu)


def _ple_kernel(x_ref, p_ref, wg_ref, wp_ref, o_ref):
    o_ref[...] = (_sigmoid(_dot(x_ref[...], wg_ref[...]))
                  * _dot(p_ref[...].astype(BF16), wp_ref[...])).astype(o_ref.dtype)


def ple_term(xb, p, wpg, wpp, *, tm=512, tn=1024):
    t, d = xb.shape
    pd = p.shape[1]
    n = wpg.shape[1]
    tm, tn = min(tm, t), min(tn, n)
    return pl.pallas_call(
        _ple_kernel,
        name="ple",
        grid=(_div(t, tm), n // tn),
        in_specs=[pl.BlockSpec((tm, d), lambda i, j: (i, 0)),
                  pl.BlockSpec((tm, pd), lambda i, j: (i, 0)),
                  pl.BlockSpec((d, tn), lambda i, j: (0, j)),
                  pl.BlockSpec((pd, tn), lambda i, j: (0, j))],
        out_specs=pl.BlockSpec((tm, tn), lambda i, j: (i, j)),
        out_shape=jax.ShapeDtypeStruct((t, n), F32),
        compiler_params=_cparams(("parallel", "parallel")),
    )(xb, p, wpg, wpp)


def _router_kernel(x_ref, w_ref, o_ref, *, tm):
    logits = jnp.dot(x_ref[...], w_ref[...], precision=HIGHEST, preferred_element_type=F32)
    lane = lax.broadcasted_iota(jnp.int32, (tm, LANES), 1)
    neg = -jnp.inf
    l1 = jnp.where(lane < N_EXPERTS, logits, neg)
    m1 = jnp.max(l1, -1, keepdims=True)
    i1 = jnp.min(jnp.where(l1 == m1, lane, LANES), -1, keepdims=True)
    l2 = jnp.where(lane == i1, neg, l1)
    m2 = jnp.max(l2, -1, keepdims=True)
    i2 = jnp.min(jnp.where(l2 == m2, lane, LANES), -1, keepdims=True)
    e = jnp.exp(m2 - m1)
    w1 = 1.0 / (1.0 + e)
    w2 = e / (1.0 + e)
    o_ref[...] = jnp.where(lane == 0, i1.astype(F32),
                           jnp.where(lane == 1, i2.astype(F32),
                                     jnp.where(lane == 2, w1, jnp.where(lane == 3, w2, 0.0))))


def router(xf, rw, *, tm=512):
    t, d = xf.shape
    tm = min(tm, t)
    return pl.pallas_call(
        functools.partial(_router_kernel, tm=tm),
        name="router",
        grid=(_div(t, tm),),
        in_specs=[pl.BlockSpec((tm, d), lambda i: (i, 0)),
                  pl.BlockSpec((d, LANES), lambda i: (0, 0))],
        out_specs=pl.BlockSpec((tm, LANES), lambda i: (i, 0)),
        out_shape=jax.ShapeDtypeStruct((t, LANES), F32),
        compiler_params=_cparams(("parallel",)),
    )(xf, rw)


def _row_copy(src_hbm, dst_vmem, sem, src_row, dst_row):
    return pltpu.make_async_copy(src_hbm.at[pl.ds(src_row, 1)], dst_vmem.at[pl.ds(dst_row, 1)], sem)


def _gather_kernel(tok_ref, x_hbm, o_ref, buf, sem, *, tg):
    i = pl.program_id(0)
    slot = i % 2

    def issue_tile(tile, sl):
        def body(r, c):
            _row_copy(x_hbm, buf.at[sl], sem.at[sl], tok_ref[tile * tg + r], r).start()
            return c

        lax.fori_loop(0, tg, body, 0, unroll=8)

    @pl.when(i == 0)
    def _():
        issue_tile(0, 0)

    @pl.when(i + 1 < pl.num_programs(0))
    def _():
        issue_tile(i + 1, 1 - slot)

    def drain(r, c):
        _row_copy(x_hbm, buf.at[slot], sem.at[slot], 0, r).wait()
        return c

    lax.fori_loop(0, tg, drain, 0, unroll=8)
    o_ref[...] = buf[slot].astype(o_ref.dtype)


def gather_rows(xf, tok, *, tg=256):
    p = tok.shape[0]
    d = xf.shape[1]
    return pl.pallas_call(
        functools.partial(_gather_kernel, tg=tg),
        name="moe_gather",
        grid_spec=pltpu.PrefetchScalarGridSpec(
            num_scalar_prefetch=1,
            grid=(_div(p, tg),),
            in_specs=[pl.BlockSpec(memory_space=pl.ANY)],
            out_specs=pl.BlockSpec((tg, d), lambda i, tok: (i, 0)),
            scratch_shapes=[pltpu.VMEM((2, tg, d), F32), pltpu.SemaphoreType.DMA((2,))]),
        out_shape=jax.ShapeDtypeStruct((p, d), BF16),
        compiler_params=_cparams(("arbitrary",)),
    )(tok, xf)


def _gswiglu_kernel(te_ref, x_ref, wg_ref, wu_ref, o_ref, *, ntile):
    i = pl.program_id(1)

    @pl.when(i < te_ref[ntile])
    def _():
        x = x_ref[...]
        o_ref[...] = (_silu(_dot(x, wg_ref[...])) * _dot(x, wu_ref[...])).astype(o_ref.dtype)

    @pl.when(i >= te_ref[ntile])
    def _():
        o_ref[...] = jnp.zeros_like(o_ref)


def grouped_swiglu_up(xs, wgu, te, *, tm, tn=512):
    p, d = xs.shape
    f = wgu.shape[2] // 2
    nf = _div(f, tn)
    ntile = _div(p, tm)
    return pl.pallas_call(
        functools.partial(_gswiglu_kernel, ntile=ntile),
        name="moe_up",
        grid_spec=pltpu.PrefetchScalarGridSpec(
            num_scalar_prefetch=1,
            grid=(nf, ntile),
            in_specs=[pl.BlockSpec((tm, d), lambda j, i, te: (i, 0)),
                      pl.BlockSpec((None, d, tn), lambda j, i, te: (te[i], 0, j)),
                      pl.BlockSpec((None, d, tn), lambda j, i, te: (te[i], 0, nf + j))],
            out_specs=pl.BlockSpec((tm, tn), lambda j, i, te: (i, j))),
        out_shape=jax.ShapeDtypeStruct((p, f), BF16),
        compiler_params=_cparams(("parallel", "parallel")),
    )(te, xs, wgu, wgu)


def _gdown_kernel(te_ref, h_ref, w_ref, o_ref, acc_ref, *, ntile, nk):
    i = pl.program_id(0)
    k = pl.program_id(1)

    @pl.when(k == 0)
    def _():
        acc_ref[...] = jnp.zeros_like(acc_ref)

    @pl.when(i < te_ref[ntile])
    def _():
        acc_ref[...] += _dot(h_ref[...], w_ref[...])

    @pl.when(k == nk - 1)
    def _():
        o_ref[...] = acc_ref[...].astype(o_ref.dtype)


def grouped_down(h, wd, te, *, tm, tk=1024):
    p, f = h.shape
    d = wd.shape[2]
    ntile = _div(p, tm)
    nk = _div(f, tk)
    return pl.pallas_call(
        functools.partial(_gdown_kernel, ntile=ntile, nk=nk),
        name="moe_down",
        grid_spec=pltpu.PrefetchScalarGridSpec(
            num_scalar_prefetch=1,
            grid=(ntile, nk),
            in_specs=[pl.BlockSpec((tm, tk), lambda i, k, te: (i, k)),
                      pl.BlockSpec((None, tk, d), lambda i, k, te: (te[i], k, 0))],
            out_specs=pl.BlockSpec((tm, d), lambda i, k, te: (i, 0)),
            scratch_shapes=[pltpu.VMEM((tm, d), F32)]),
        out_shape=jax.ShapeDtypeStruct((p, d), F32),
        compiler_params=_cparams(("parallel", "arbitrary")),
    )(te, h, wd)


def _combine_ln_kernel(s1_ref, s2_ref, y_hbm, x_ref, ple_ref, rt_ref, g_ref, b_ref, of_ref, ob_ref,
                       buf1, buf2, sem, *, tc, alpha):
    base = pl.program_id(0) * tc

    def issue(r, c):
        _row_copy(y_hbm, buf1, sem, s1_ref[base + r], r).start()
        _row_copy(y_hbm, buf2, sem, s2_ref[base + r], r).start()
        return c

    lax.fori_loop(0, tc, issue, 0)

    def drain(r, c):
        _row_copy(y_hbm, buf1, sem, 0, r).wait()
        _row_copy(y_hbm, buf2, sem, 0, r).wait()
        return c

    lax.fori_loop(0, tc, drain, 0)
    rt = rt_ref[...]
    ff = rt[:, 2:3] * buf1[...] + rt[:, 3:4] * buf2[...]
    y = _ln_rows(alpha * x_ref[...] + ff + ple_ref[...], g_ref[...], b_ref[...])
    of_ref[...] = y
    ob_ref[...] = y.astype(BF16)


def combine_ln(ys, s1, s2, x, ple, rt, g, b, *, alpha, tc=256):
    t, d = x.shape
    tc = min(tc, t)
    row = pl.BlockSpec((tc, d), lambda i, a, c: (i, 0))
    vec = pl.BlockSpec((1, d), lambda i, a, c: (0, 0))
    return pl.pallas_call(
        functools.partial(_combine_ln_kernel, tc=tc, alpha=alpha),
        name="moe_combine_ln",
        grid_spec=pltpu.PrefetchScalarGridSpec(
            num_scalar_prefetch=2,
            grid=(_div(t, tc),),
            in_specs=[pl.BlockSpec(memory_space=pl.ANY), row, row,
                      pl.BlockSpec((tc, LANES), lambda i, a, c: (i, 0)), vec, vec],
            out_specs=[row, row],
            scratch_shapes=[pltpu.VMEM((tc, d), F32), pltpu.VMEM((tc, d), F32),
                            pltpu.SemaphoreType.DMA(())]),
        out_shape=[jax.ShapeDtypeStruct((t, d), F32), jax.ShapeDtypeStruct((t, d), BF16)],
        compiler_params=_cparams(("arbitrary",)),
    )(s1, s2, ys, x, ple, rt, g.reshape(1, d), b.reshape(1, d))


def _routing_tables(rt, *, tm):
    t = rt.shape[0]
    e_flat = rt[:, :TOP_K].astype(jnp.int32).reshape(-1)
    na = e_flat.shape[0]
    p = na + N_EXPERTS * tm
    ntile = _div(p, tm)
    onehot = (e_flat[:, None] == jnp.arange(N_EXPERTS, dtype=jnp.int32)[None, :]).astype(jnp.int32)
    csum = jnp.cumsum(onehot, axis=0)
    counts = csum[-1]
    rank = jnp.sum((csum - onehot) * onehot, axis=1)
    padded = ((counts + tm - 1) // tm) * tm
    ends = jnp.cumsum(padded)
    starts = ends - padded
    slot = starts[e_flat] + rank
    tok = jnp.zeros((p,), jnp.int32).at[slot].set(jnp.arange(na, dtype=jnp.int32) // TOP_K)
    tile_start = jnp.arange(ntile, dtype=jnp.int32) * tm
    te = jnp.minimum(jnp.sum((tile_start[:, None] >= ends[None, :]).astype(jnp.int32), axis=1),
                     N_EXPERTS - 1)
    te = jnp.concatenate([te, (ends[-1] // tm)[None]]).astype(jnp.int32)
    slot2 = slot.reshape(t, TOP_K)
    return tok, te, slot2[:, 0], slot2[:, 1]


def kernel(x, p, emb_ln_g, emb_ln_b, w_in, conv_w, a_log, dt_bias, o_norm_w, w_fourier, w_delta, w_out,
           ln1_g, ln1_b, ffn_gate_up, ffn_down, router_w, exp_gate_up, exp_down, ple_gate, ple_proj,
           ln2_g, ln2_b):
    batch, seq, d = x.shape
    depth = w_in.shape[0]
    t = batch * seq
    alpha = float((2 * depth) ** 0.25)
    nchunk = _div(seq, CHUNK)
    main_w = FOURIER_WIDTH + 4 * DN_WIDTH
    small0 = main_w
    gate0 = main_w + 4 * DN_HEADS
    moe_tm = 512

    chan_t, seq_t = _dft_tables(seq)
    xf, xb = layer_norm(x.reshape(t, d), emb_ln_g, emb_ln_b)
    pad_lanes = lambda v: jnp.pad(v, ((0, 0), (0, LANES - v.shape[1])))

    for i in range(depth):
        w_main = w_in[i, :, :main_w].astype(BF16)
        w_gate = w_in[i, :, gate0:].astype(BF16)
        w_beta = pad_lanes(w_in[i, :, small0:small0 + 2 * DN_HEADS])
        w_a = pad_lanes(w_in[i, :, small0 + 2 * DN_HEADS:gate0])

        proj = matmul(xb, w_main, out_dtype=F32, tm=1024, tn=1024, name="proj_main")
        gates_fd = matmul(xb, w_gate, out_dtype=F32, tm=1024, tn=1024, name="proj_gates")
        beta, gc = dn_gates(xf, w_beta, w_a, pad_lanes(a_log[i].reshape(1, -1)),
                            pad_lanes(dt_bias[i].reshape(1, -1)))
        gct = gc[:, :2 * DN_HEADS].reshape(batch, seq, 2 * DN_HEADS).transpose(0, 2, 1)
        gct = gct.reshape(batch, 2 * DN_HEADS, nchunk, CHUNK)

        vch = chan_dft(proj, chan_t, batch=batch, seq=seq)
        a_f = matmul(seq_t, vch.reshape(batch, 2 * seq, FOURIER_WIDTH), out_dtype=BF16,
                     tm=1024, tn=1024, tk=2048, name="seq_dft").reshape(t, FOURIER_WIDTH)

        qkvn = dn_prep(proj, conv_w[i], batch=batch, seq=seq, col0=FOURIER_WIDTH)
        a_d = dn_core(qkvn, proj, beta, gc, gct, o_norm_w[i].reshape(1, -1), batch=batch, seq=seq,
                      z_col0=FOURIER_WIDTH + 3 * DN_WIDTH)

        merged = merge_branches(a_f, a_d, w_fourier[i].astype(BF16), w_delta[i].astype(BF16), gates_fd)
        xf, xb = mm_res_ln(merged, w_out[i].astype(BF16), xf, ln1_g[i], ln1_b[i], alpha=alpha)

        ple = ple_term(xb, p[i].reshape(t, -1), ple_gate[i].astype(BF16), ple_proj[i].astype(BF16))
        if i % 2 == 0:
            hmid = swiglu_up(xb, ffn_gate_up[i // 2].astype(BF16))
            xf, xb = mm_res_ln(hmid, ffn_down[i // 2].astype(BF16), xf, ln2_g[i], ln2_b[i],
                               alpha=alpha, extra=ple)
        else:
            j = i // 2
            rt = router(xf, pad_lanes(router_w[j]))
            tok, te, s1, s2 = _routing_tables(rt, tm=moe_tm)
            xs = gather_rows(xf, tok)
            hmid = grouped_swiglu_up(xs, exp_gate_up[j].astype(BF16), te, tm=moe_tm)
            ys = grouped_down(hmid, exp_down[j].astype(BF16), te, tm=moe_tm)
            xf, xb = combine_ln(ys, s1, s2, xf, ple, rt, ln2_g[i], ln2_b[i], alpha=alpha)
    return xf.reshape(batch, seq, d)
```
